```python
import jax, jax.numpy as jnp
from jax import lax
import numpy as np

D_MODEL = 1024
BATCH = 1
SEQ = 16384
DEPTH = 1
DEC_BATCH = 32
DEC_SEQ = 8
PAST_LEN = 16384
PAGE_SIZE = 128

H_RET = D_MODEL // 256
DK_RET = 128
DV_RET = 128
H_SB = D_MODEL // 128
D_SB = 64
RET_W = H_RET * DV_RET
SB_W = H_SB * D_SB
MIX_W = RET_W + SB_W
IN_W = 4 * RET_W + 3 * SB_W
D_FF = ((8 * D_MODEL // 3 + 127) // 128) * 128
CONV_W = 3
BLOCK = 128
ROPE_BASE = 10000.0
EPS = 1e-6
N_ADA = 6
SB_BIAS_INIT = -8.0

kernel_name = "hymba_retention_stickbreaking_convffn_step"


def _block(t):
    return BLOCK if t % BLOCK == 0 else t


def rmsnorm(x, g):
    xf = x.astype(jnp.float32)
    y = xf * lax.rsqrt(jnp.mean(xf * xf, axis=-1, keepdims=True) + EPS) * g.astype(jnp.float32)
    return y.astype(x.dtype)


def rotary(x, pos):
    half = x.shape[-1] // 2
    inv = ROPE_BASE ** (-jnp.arange(half, dtype=jnp.float32) / half)
    ang = pos.astype(jnp.float32)[:, None] * inv[None, :]
    cos, sin = jnp.cos(ang), jnp.sin(ang)
    xf = x.astype(jnp.float32)
    x1, x2 = xf[..., :half], xf[..., half:]
    return jnp.concatenate([x1 * cos - x2 * sin, x1 * sin + x2 * cos], axis=-1)


def heads(t, h, d):
    b, n, _ = t.shape
    return t.reshape(b, n, h, d).transpose(0, 2, 1, 3)


def retention(q, k, v, s0):
    b, h, t, dk = q.shape
    dv = v.shape[-1]
    lg = jnp.log(1.0 - 2.0 ** (-5.0 - jnp.arange(H_RET, dtype=jnp.float32)))
    c = _block(t)
    n = t // c
    idx = jnp.arange(c)
    diff = (idx[:, None] - idx[None, :]).astype(jnp.float32)
    decay = jnp.where(diff[None] >= 0, jnp.exp(jnp.maximum(diff, 0.0)[None] * lg[:, None, None]), 0.0)
    q_decay = jnp.exp((idx + 1).astype(jnp.float32)[None, :] * lg[:, None])
    k_decay = jnp.exp((c - 1 - idx).astype(jnp.float32)[None, :] * lg[:, None])
    chunk_decay = jnp.exp(c * lg)

    def step(s, inp):
        qc, kc, vc = inp
        scores = jnp.einsum('bhnd,bhmd->bhnm', qc, kc) * decay[None]
        o = (jnp.einsum('bhnm,bhmv->bhnv', scores, vc)
             + jnp.einsum('bhnd,bhdv->bhnv', qc * q_decay[None, :, :, None], s))
        s = (s * chunk_decay[None, :, None, None]
             + jnp.einsum('bhmd,bhmv->bhdv', kc * k_decay[None, :, :, None], vc))
        return s, o

    qx = jnp.moveaxis(q.reshape(b, h, n, c, dk), 2, 0)
    kx = jnp.moveaxis(k.reshape(b, h, n, c, dk), 2, 0)
    vx = jnp.moveaxis(v.reshape(b, h, n, c, dv), 2, 0)
    s, o = lax.scan(step, s0.astype(jnp.float32), (qx, kx, vx))
    o = jnp.moveaxis(o, 0, 2).reshape(b, h, t, dv)
    return o, s


def sb_attention(q, k, v, bias, q_pos0):
    b, h, tq, d = q.shape
    tk = k.shape[2]
    bq = _block(tq)
    nb = tq // bq
    scale = d ** -0.5
    kf = k.astype(jnp.float32)
    vf = v.astype(jnp.float32)
    bf = bias.astype(jnp.float32)[None, :, None, None]
    qb = jnp.moveaxis(q.astype(jnp.float32).reshape(b, h, nb, bq, d), 2, 0)
    starts = q_pos0 + jnp.arange(nb) * bq
    spos = jnp.arange(tk)

    def blk(args):
        qi, t0 = args
        z = jnp.einsum('bhqd,bhkd->bhqk', qi, kf) * scale + bf
        tpos = t0 + jnp.arange(bq)
        mask = spos[None, :] < tpos[:, None]
        log_om = jnp.where(mask, jax.nn.log_sigmoid(-z), 0.0)
        after = lax.cumsum(log_om, axis=3, reverse=True)
        after_excl = jnp.concatenate([after[..., 1:], jnp.zeros_like(after[..., :1])], axis=-1)
        a = jnp.where(mask, jnp.exp(jax.nn.log_sigmoid(z) + after_excl), 0.0)
        return jnp.einsum('bhqk,bhkd->bhqd', a, vf)

    o = lax.map(blk, (qb, starts))
    return jnp.moveaxis(o, 0, 2).reshape(b, h, tq, d)


def layer(x, c, pos, q_pos0, ret_s0, k_past, v_past, conv_past,
          w_ada, b_ada, norm_mix, w_in, ret_gn, sb_gn, sb_bias, w_out,
          norm_ffn, w_up_gate, w_up_val, conv_w, conv_b, w_down):
    b, t, _ = x.shape
    ada = jax.nn.silu(c.astype(jnp.float32)) @ w_ada.astype(jnp.float32) + b_ada.astype(jnp.float32)
    ada = ada.astype(x.dtype)[:, None, :]
    sh_m, sc_m, g_m, sh_f, sc_f, g_f = jnp.split(ada, N_ADA, axis=-1)

    h = rmsnorm(x, norm_mix) * (1 + sc_m) + sh_m
    proj = h @ w_in
    cuts = [RET_W, 2 * RET_W, 3 * RET_W, 4 * RET_W, 4 * RET_W + SB_W, 4 * RET_W + 2 * SB_W]
    rq, rk, rv, rg, sq, sk, sv = jnp.split(proj, cuts, axis=-1)

    q = rotary(heads(rq, H_RET, DK_RET), pos)
    k = rotary(heads(rk, H_RET, DK_RET), pos) * (DK_RET ** -0.5)
    v = heads(rv, H_RET, DV_RET).astype(jnp.float32)
    ro, ret_s = retention(q, k, v, ret_s0)
    mu = jnp.mean(ro, axis=-1, keepdims=True)
    var = jnp.mean(jnp.square(ro - mu), axis=-1, keepdims=True)
    ro = (ro - mu) * lax.rsqrt(var + EPS) * ret_gn.astype(jnp.float32).reshape(H_RET, DV_RET)[None, :, None, :]
    ro = ro.transpose(0, 2, 1, 3).reshape(b, t, RET_W) * jax.nn.silu(rg.astype(jnp.float32))

    qs = heads(sq, H_SB, D_SB)
    ks = heads(sk, H_SB, D_SB)
    vs = heads(sv, H_SB, D_SB)
    k_all = jnp.concatenate([k_past, ks.astype(k_past.dtype)], axis=2)
    v_all = jnp.concatenate([v_past, vs.astype(v_past.dtype)], axis=2)
    so = sb_attention(qs, k_all, v_all, sb_bias, q_pos0)
    so = so * lax.rsqrt(jnp.mean(so * so, axis=-1, keepdims=True) + EPS) \
        * sb_gn.astype(jnp.float32).reshape(H_SB, D_SB)[None, :, None, :]
    so = so.transpose(0, 2, 1, 3).reshape(b, t, SB_W)

    mix = jnp.concatenate([ro, so], axis=-1).astype(x.dtype) @ w_out
    x = x + g_m * mix

    h = rmsnorm(x, norm_ffn) * (1 + sc_f) + sh_f
    a = h @ w_up_gate
    bval = h @ w_up_val
    a_ext = jnp.concatenate([conv_past.astype(a.dtype), a], axis=1)
    conv = conv_b
    for j in range(CONV_W):
        conv = conv + a_ext[:, j:j + t] * conv_w[j]
    ff = (jax.nn.silu(conv) * bval) @ w_down
    x = x + g_f * ff
    return (x, ks.transpose(0, 2, 1, 3), vs.transpose(0, 2, 1, 3), ret_s, a_ext[:, t:])


def setup_inputs(seed: int = 0) -> dict:
    key = jax.random.key(seed)
    ks = jax.random.split(key, 24)
    f32 = jnp.float32
    n_pages = PAST_LEN // PAGE_SIZE
    n_used = DEC_BATCH * n_pages
    n_phys = n_used + max(1, n_used // 4)

    def nrm(k, shape, s):
        return jax.random.normal(k, shape, f32) * s

    page_table = jax.random.permutation(ks[4], n_phys)[:n_used].reshape(DEC_BATCH, n_pages).astype(jnp.int32)
    return {
        'x_prompt': nrm(ks[0], (BATCH, SEQ, D_MODEL), 1.0),
        'x_sample': nrm(ks[1], (DEC_BATCH, DEC_SEQ, D_MODEL), 1.0),
        'cache_k_pages': nrm(ks[2], (DEPTH, n_phys, PAGE_SIZE, H_SB, D_SB), 1.0),
        'cache_v_pages': nrm(ks[3], (DEPTH, n_phys, PAGE_SIZE, H_SB, D_SB), 1.0),
        'page_table': page_table,
        'state_ret': nrm(ks[5], (DEPTH, DEC_BATCH, H_RET, DK_RET, DV_RET), 0.5),
        'state_conv': nrm(ks[6], (DEPTH, DEC_BATCH, CONV_W - 1, D_FF), 1.0),
        'c_prompt': nrm(ks[7], (BATCH, D_MODEL), 1.0),
        'c_sample': nrm(ks[8], (DEC_BATCH, D_MODEL), 1.0),
        'w_ada': nrm(ks[9], (DEPTH, D_MODEL, N_ADA * D_MODEL), 0.5 * D_MODEL ** -0.5),
        'b_ada': nrm(ks[10], (DEPTH, N_ADA * D_MODEL), 0.02),
        'norm_mix': 1.0 + nrm(ks[11], (DEPTH, D_MODEL), 0.02),
        'w_in': nrm(ks[12], (DEPTH, D_MODEL, IN_W), D_MODEL ** -0.5),
        'ret_gn': 1.0 + nrm(ks[13], (DEPTH, RET_W), 0.02),
        'sb_gn': 1.0 + nrm(ks[14], (DEPTH, SB_W), 0.02),
        'sb_bias': SB_BIAS_INIT + nrm(ks[23], (DEPTH, H_SB), 0.1),
        'w_out': nrm(ks[15], (DEPTH, MIX_W, D_MODEL), MIX_W ** -0.5),
        'norm_ffn': 1.0 + nrm(ks[16], (DEPTH, D_MODEL), 0.02),
        'w_up_gate': nrm(ks[17], (DEPTH, D_MODEL, D_FF), D_MODEL ** -0.5),
        'w_up_val': nrm(ks[18], (DEPTH, D_MODEL, D_FF), D_MODEL ** -0.5),
        'conv_w': nrm(ks[19], (DEPTH, CONV_W, D_FF), CONV_W ** -0.5),
        'conv_b': nrm(ks[20], (DEPTH, D_FF), 0.02),
        'w_down': nrm(ks[21], (DEPTH, D_FF, D_MODEL), D_FF ** -0.5),
        'norm_final': 1.0 + nrm(ks[22], (D_MODEL,), 0.02),
    }


def reference(x_prompt, x_sample, cache_k_pages, cache_v_pages, page_table, state_ret, state_conv,
              c_prompt, c_sample, w_ada, b_ada, norm_mix, w_in, ret_gn, sb_gn, sb_bias, w_out,
              norm_ffn, w_up_gate, w_up_val, conv_w, conv_b, w_down, norm_final):
    b, t, _ = x_prompt.shape
    db, ts, _ = x_sample.shape
    past = page_table.shape[1] * cache_k_pages.shape[2]
    pos_p = jnp.arange(t)
    pos_s = past + jnp.arange(ts)
    xp, xs = x_prompt, x_sample
    kp_l, vp_l, ks_l, vs_l, rp_l, rs_l, cp_l, cs_l = [], [], [], [], [], [], [], []
    for l in range(DEPTH):
        w = (w_ada[l], b_ada[l], norm_mix[l], w_in[l], ret_gn[l], sb_gn[l], sb_bias[l], w_out[l],
             norm_ffn[l], w_up_gate[l], w_up_val[l], conv_w[l], conv_b[l], w_down[l])
        xp, kp, vp, rp, cp = layer(
            xp, c_prompt, pos_p, 0,
            jnp.zeros((b, H_RET, DK_RET, DV_RET), jnp.float32),
            jnp.zeros((b, H_SB, 0, D_SB), x_prompt.dtype),
            jnp.zeros((b, H_SB, 0, D_SB), x_prompt.dtype),
            jnp.zeros((b, CONV_W - 1, D_FF), x_prompt.dtype), *w)
        k_past = cache_k_pages[l][page_table].reshape(db, past, H_SB, D_SB).transpose(0, 2, 1, 3)
        v_past = cache_v_pages[l][page_table].reshape(db, past, H_SB, D_SB).transpose(0, 2, 1, 3)
        xs, ks_, vs_, rs, cs = layer(
            xs, c_sample, pos_s, past, state_ret[l], k_past, v_past, state_conv[l], *w)
        kp_l.append(kp); vp_l.append(vp); ks_l.append(ks_); vs_l.append(vs_)
        rp_l.append(rp); rs_l.append(rs); cp_l.append(cp); cs_l.append(cs)
    y_prompt = rmsnorm(xp, norm_final)
    y_sample = rmsnorm(xs, norm_final)
    return (y_prompt, y_sample,
            jnp.stack(kp_l), jnp.stack(vp_l), jnp.stack(ks_l), jnp.stack(vs_l),
            jnp.stack(rp_l), jnp.stack(rs_l), jnp.stack(cp_l), jnp.stack(cs_l))
```

```python
from functools import partial

import jax
import jax.numpy as jnp
from jax import lax
from jax.experimental import pallas as pl
from jax.experimental.pallas import tpu as pltpu

F32 = jnp.float32
BF16 = jnp.bfloat16

D_MODEL = 1024
H_RET, DK_RET, DV_RET = 4, 128, 128
H_SB, D_SB = 8, 64
RET_W = H_RET * DV_RET
SB_W = H_SB * D_SB
IN_W = 4 * RET_W + 3 * SB_W
D_FF = 2816
CONV_W = 3
N_ADA = 6
ROPE_BASE = 10000.0
EPS = 1e-6
RET_CHUNK = 128

SUBLANES = 8
LANES = 128
VMEM_LIMIT_BYTES = 56 * 1024 * 1024


def _params(*semantics):
    return pltpu.CompilerParams(dimension_semantics=semantics, vmem_limit_bytes=VMEM_LIMIT_BYTES)


def _silu(x):
    return x / (1.0 + jnp.exp(-x))


def _rms(x, g):
    return x * lax.rsqrt(jnp.mean(x * x, axis=-1, keepdims=True) + EPS) * g


def _row_spec(rows_total, tile, width):
    if rows_total == 1:
        return pl.BlockSpec((1, width), lambda i: (0, 0))
    return pl.BlockSpec((tile, width), lambda i: (i, 0))


def _const_spec(shape):
    return pl.BlockSpec(shape, lambda *_: (0,) * len(shape))


def _ada_kernel(c_ref, w_ref, b_ref, o_ref):
    s = _silu(c_ref[...])
    o_ref[...] = jnp.dot(s, w_ref[...], preferred_element_type=F32,
                         precision=lax.Precision.HIGHEST) + b_ref[...]


def _ada_call(c, w_ada, b_ada):
    rows = c.shape[0]
    tn = 1536
    return pl.pallas_call(
        _ada_kernel,
        grid=(N_ADA * D_MODEL // tn,),
        in_specs=[_const_spec((rows, D_MODEL)),
                  pl.BlockSpec((D_MODEL, tn), lambda j: (0, j)),
                  pl.BlockSpec((1, tn), lambda j: (0, j))],
        out_specs=pl.BlockSpec((rows, tn), lambda j: (0, j)),
        out_shape=jax.ShapeDtypeStruct((rows, N_ADA * D_MODEL), F32),
        compiler_params=_params("parallel"),
        name="ada",
    )(c, w_ada, b_ada.reshape(1, -1))


def _inproj_kernel(x_ref, sc_ref, sh_ref, nw_ref, w_ref, cos_ref, sin_ref,
                   rq_ref, rk_ref, rv_ref, rg_ref, sq_ref, sk_ref, sv_ref, *extra, transposed):
    x = x_ref[...]
    h = _rms(x, nw_ref[...]) * (1.0 + sc_ref[...]) + sh_ref[...]
    proj = jnp.dot(h.astype(BF16), w_ref[...], preferred_element_type=F32)
    cos = cos_ref[...]
    sin = sin_ref[...]

    def rope(t):
        parts = []
        for hh in range(H_RET):
            th = t[:, hh * DK_RET:(hh + 1) * DK_RET]
            parts.append(th * cos + pltpu.roll(th, DK_RET // 2, 1) * sin)
        return jnp.concatenate(parts, axis=1)

    rq_ref[...] = rope(proj[:, 0:RET_W]).astype(rq_ref.dtype)
    rk_ref[...] = (rope(proj[:, RET_W:2 * RET_W]) * (DK_RET ** -0.5)).astype(rk_ref.dtype)
    rv_ref[...] = proj[:, 2 * RET_W:3 * RET_W].astype(rv_ref.dtype)
    rg_ref[...] = proj[:, 3 * RET_W:4 * RET_W]
    o = 4 * RET_W
    sq = proj[:, o:o + SB_W] * (D_SB ** -0.5)
    sk = proj[:, o + SB_W:o + 2 * SB_W]
    sv = proj[:, o + 2 * SB_W:o + 3 * SB_W]
    sk_ref[...] = sk
    sv_ref[...] = sv
    if transposed:
        sqt_ref, skb_ref, svt_ref = extra
        sq_ref[...] = sq
        sqt_ref[...] = sq.T.astype(BF16)
        skb_ref[...] = sk.astype(BF16)
        svt_ref[...] = sv.T.astype(BF16)
    else:
        sq_ref[...] = sq


def _inproj_call(x, sc, sh, norm_w, w_in, cos, sin, *, tile, transposed):
    rows = x.shape[0]
    n = rows // tile
    row = lambda w: pl.BlockSpec((tile, w), lambda i: (i, 0))
    col = lambda w: pl.BlockSpec((w, tile), lambda i: (0, i))
    out_specs = [row(RET_W)] * 4 + [row(SB_W)] * 3
    ret_dtype = BF16 if transposed else F32
    out_shape = ([jax.ShapeDtypeStruct((rows, RET_W), ret_dtype)] * 3
                 + [jax.ShapeDtypeStruct((rows, RET_W), F32)]
                 + [jax.ShapeDtypeStruct((rows, SB_W), F32)] * 3)
    if transposed:
        out_specs += [col(SB_W), row(SB_W), col(SB_W)]
        out_shape += [jax.ShapeDtypeStruct((SB_W, rows), BF16),
                      jax.ShapeDtypeStruct((rows, SB_W), BF16),
                      jax.ShapeDtypeStruct((SB_W, rows), BF16)]
    return pl.pallas_call(
        partial(_inproj_kernel, transposed=transposed),
        grid=(n,),
        in_specs=[row(D_MODEL), _row_spec(sc.shape[0], tile, D_MODEL),
                  _row_spec(sh.shape[0], tile, D_MODEL), _const_spec((1, D_MODEL)),
                  _const_spec((D_MODEL, IN_W)), row(DK_RET), row(DK_RET)],
        out_specs=out_specs,
        out_shape=out_shape,
        compiler_params=_params("parallel"),
        name="inproj",
    )(x, sc, sh, norm_w.reshape(1, -1), w_in, cos, sin)


def _retention_kernel(q_ref, k_ref, v_ref, g_ref, s0_ref, gn_ref, o_ref, s_out_ref, s_scr, *, chunk):
    c = chunk
    ci = pl.program_id(1)

    @pl.when(ci == 0)
    def _():
        s_scr[...] = s0_ref[0]

    row = lax.broadcasted_iota(jnp.int32, (c, c), 0).astype(F32)
    colv = lax.broadcasted_iota(jnp.int32, (c, c), 1).astype(F32)
    diff = row - colv
    ridx = lax.broadcasted_iota(jnp.int32, (c, 1), 0).astype(F32)
    outs = []
    for hh in range(H_RET):
        lg = jnp.log(1.0 - 2.0 ** (-5.0 - hh))
        decay = jnp.where(diff >= 0, jnp.exp(jnp.maximum(diff, 0.0) * lg), 0.0)
        q_decay = jnp.exp((ridx + 1.0) * lg)
        k_decay = jnp.exp((c - 1.0 - ridx) * lg)
        chunk_decay = jnp.exp(c * lg)
        sl = slice(hh * DK_RET, (hh + 1) * DK_RET)
        qh = q_ref[:, sl].astype(BF16)
        kh = k_ref[:, sl].astype(BF16)
        vh = v_ref[:, sl].astype(BF16)
        s = s_scr[hh]
        scores = lax.dot_general(qh, kh, (((1,), (1,)), ((), ())), preferred_element_type=F32) * decay
        o = jnp.dot(scores.astype(BF16), vh, preferred_element_type=F32)
        o = o + jnp.dot((qh.astype(F32) * q_decay).astype(BF16), s.astype(BF16), preferred_element_type=F32)
        kd = (kh.astype(F32) * k_decay).astype(BF16)
        s_new = s * chunk_decay + lax.dot_general(kd, vh, (((0,), (0,)), ((), ())), preferred_element_type=F32)
        s_scr[hh] = s_new
        s_out_ref[0, hh] = s_new
        mu = jnp.mean(o, axis=-1, keepdims=True)
        d = o - mu
        var = jnp.mean(d * d, axis=-1, keepdims=True)
        y = d * lax.rsqrt(var + EPS) * gn_ref[:, sl]
        outs.append(y * _silu(g_ref[:, sl]))
    o_ref[...] = jnp.concatenate(outs, axis=1).astype(o_ref.dtype)


def _retention_call(q, k, v, g, s0, ret_gn, *, batch, seq):
    chunk = RET_CHUNK if seq % RET_CHUNK == 0 else seq
    n = seq // chunk
    row = pl.BlockSpec((chunk, RET_W), lambda b, i: (b * n + i, 0))
    st = pl.BlockSpec((1, H_RET, DK_RET, DV_RET), lambda b, i: (b, 0, 0, 0))
    return pl.pallas_call(
        partial(_retention_kernel, chunk=chunk),
        grid=(batch, n),
        in_specs=[row, row, row, row, st, pl.BlockSpec((1, RET_W), lambda b, i: (0, 0))],
        out_specs=[row, st],
        out_shape=[jax.ShapeDtypeStruct((batch * seq, RET_W), q.dtype),
                   jax.ShapeDtypeStruct((batch, H_RET, DK_RET, DV_RET), F32)],
        scratch_shapes=[pltpu.VMEM((H_RET, DK_RET, DV_RET), F32)],
        compiler_params=_params("parallel", "arbitrary"),
        name="retention",
    )(q, k, v, g, s0, ret_gn.reshape(1, -1))


def _suffix_matrix(n):
    r = lax.broadcasted_iota(jnp.int32, (n + SUBLANES, n), 0)
    c = lax.broadcasted_iota(jnp.int32, (n + SUBLANES, n), 1)
    return jnp.where((c > r) | (r >= n), 1.0, 0.0).astype(BF16)


def _sb_weights(z, suffix, carry, valid):
    n = z.shape[0]
    lg = jnp.log(1.0 + jnp.exp(-jnp.abs(z)))
    log_om = jnp.minimum(-z, 0.0) - lg
    if valid is not None:
        log_om = jnp.where(valid, log_om, 0.0)
    sums = jnp.dot(suffix, log_om.astype(BF16), preferred_element_type=F32)
    a = jnp.exp(jnp.minimum(z, 0.0) - lg + sums[:n] + carry)
    if valid is not None:
        a = jnp.where(valid, a, 0.0)
    return a, carry + sums[n:n + 1]


def _sb_prompt_kernel(bias_ref, qt_ref, k_ref, vt_ref, suf_ref, gn_ref, o_ref, acc_ref, *, bq, bk):
    p = pl.program_id(0)
    qi = pl.program_id(1)
    suffix = suf_ref[...]
    qt = qt_ref[...]
    head_of_row = lax.broadcasted_iota(jnp.int32, (2 * D_SB, bq), 0) // D_SB
    krow = lax.broadcasted_iota(jnp.int32, (bk, bq), 0)
    qcol = lax.broadcasted_iota(jnp.int32, (bk, bq), 1)
    zero = jnp.zeros((), BF16)

    for s in range(2):
        bias = bias_ref[2 * p + s]
        qm = jnp.where(head_of_row == s, qt, zero)
        rows = slice(s * D_SB, (s + 1) * D_SB)
        acc_ref[rows, :] = jnp.zeros((D_SB, bq), F32)

        def block(kb, carry, valid):
            k0 = pl.multiple_of(kb * bk, bk)
            z = jnp.dot(k_ref[pl.ds(k0, bk), :], qm, preferred_element_type=F32) + bias
            a, carry = _sb_weights(z, suffix, carry, valid)
            acc_ref[rows, :] += jnp.dot(vt_ref[rows, pl.ds(k0, bk)], a.astype(BF16),
                                        preferred_element_type=F32)
            return carry

        r = bq // bk
        carry = jnp.zeros((1, bq), F32)
        for d in reversed(range(r)):
            valid = (krow + d * bk) < qcol
            carry = block(qi * r + d, carry, valid)
        lax.fori_loop(0, qi * r, lambda j, c: block(qi * r - 1 - j, c, None), carry)

    parts = []
    for s in range(2):
        blk = acc_ref[s * D_SB:(s + 1) * D_SB, :]
        ms = jnp.mean(blk * blk, axis=0, keepdims=True)
        parts.append(blk * lax.rsqrt(ms + EPS))
    o_ref[...] = (jnp.concatenate(parts, axis=0).T * gn_ref[...]).astype(BF16)


def _sb_prompt_call(bias, qt, kb, vt, sb_gn, *, bq, bk):
    seq = kb.shape[0]
    pairs = H_SB // 2
    w = 2 * D_SB
    return pl.pallas_call(
        partial(_sb_prompt_kernel, bq=bq, bk=bk),
        grid_spec=pltpu.PrefetchScalarGridSpec(
            num_scalar_prefetch=0,
            grid=(pairs, seq // bq),
            in_specs=[pl.BlockSpec(memory_space=pltpu.SMEM),
                      pl.BlockSpec((w, bq), lambda p, i: (p, i)),
                      pl.BlockSpec((seq, w), lambda p, i: (0, p)),
                      pl.BlockSpec((w, seq), lambda p, i: (p, 0)),
                      pl.BlockSpec((bk + SUBLANES, bk), lambda p, i: (0, 0)),
                      pl.BlockSpec((1, w), lambda p, i: (0, p))],
            out_specs=pl.BlockSpec((bq, w), lambda p, i: (i, p)),
            scratch_shapes=[pltpu.VMEM((w, bq), F32)]),
        out_shape=jax.ShapeDtypeStruct((seq, SB_W), BF16),
        compiler_params=_params("parallel", "arbitrary"),
        name="sb_prompt",
    )(bias, qt, kb, vt, _suffix_matrix(bk), sb_gn.reshape(1, -1))


PAGES_PER_STEP = 4


def _sb_paged_kernel(pt_ref, qbd_ref, bias_ref, kn_ref, vn_ref, sufn_ref, suf_ref, gmat_ref, gn_ref,
                     *refs, n_new, page, pps):
    k_refs = refs[:pps]
    v_refs = refs[pps:2 * pps]
    o_ref = refs[2 * pps]
    acc_ref, carry_ref = refs[2 * pps + 1:]
    j = pl.program_id(1)
    qbd = qbd_ref[0]
    bias = bias_ref[...]
    nq = H_SB * n_new

    def attend(keys, vals, suffix, valid):
        z = jnp.dot(keys.astype(BF16), qbd, preferred_element_type=F32) + bias
        a, carry = _sb_weights(z, suffix, carry_ref[...], valid)
        carry_ref[...] = carry
        at = a.T[:nq].astype(BF16)
        acc_ref[...] += jnp.dot(at, vals.astype(BF16), preferred_element_type=F32)

    @pl.when(j == 0)
    def _():
        acc_ref[...] = jnp.zeros_like(acc_ref)
        carry_ref[...] = jnp.zeros_like(carry_ref)
        kidx = lax.broadcasted_iota(jnp.int32, (n_new, LANES), 0)
        qidx = lax.broadcasted_iota(jnp.int32, (n_new, LANES), 1) % n_new
        attend(kn_ref[0], vn_ref[0], sufn_ref[...], kidx < qidx)

    for u in range(pps):
        attend(k_refs[u][0], v_refs[u][0], suf_ref[...], None)

    @pl.when(j == pl.num_programs(1) - 1)
    def _():
        acc = acc_ref[...]
        lane_head = lax.broadcasted_iota(jnp.int32, (n_new, SB_W), 1) // D_SB
        so = jnp.zeros((n_new, SB_W), F32)
        for hh in range(H_SB):
            so = so + jnp.where(lane_head == hh, acc[hh * n_new:(hh + 1) * n_new, :], 0.0)
        sq = so * so
        hi = sq.astype(BF16)
        lo = (sq - hi.astype(F32)).astype(BF16)
        ms = (jnp.dot(hi, gmat_ref[...], preferred_element_type=F32)
              + jnp.dot(lo, gmat_ref[...], preferred_element_type=F32)) * (1.0 / D_SB)
        o_ref[0] = (so * lax.rsqrt(ms + EPS) * gn_ref[...]).astype(BF16)


def _sb_paged_call(page_table, qbd, bias_row, k_new, v_new, k_pages, v_pages, sb_gn, *, pps):
    batch, n_pages = page_table.shape
    n_new = k_new.shape[1]
    page = k_pages.shape[1]
    steps = n_pages // pps
    nq = H_SB * n_new
    lane_head = jnp.arange(SB_W) // D_SB
    gmat = (lane_head[:, None] == lane_head[None, :]).astype(BF16)

    def page_spec(u):
        return pl.BlockSpec((1, page, SB_W),
                            lambda b, j, pt: (pt[b, n_pages - 1 - (j * pps + u)], 0, 0))

    const = lambda shape: pl.BlockSpec(shape, lambda b, j, pt: (0,) * len(shape))
    per_b = lambda shape: pl.BlockSpec(shape, lambda b, j, pt: (b,) + (0,) * (len(shape) - 1))
    return pl.pallas_call(
        partial(_sb_paged_kernel, n_new=n_new, page=page, pps=pps),
        grid_spec=pltpu.PrefetchScalarGridSpec(
            num_scalar_prefetch=1,
            grid=(batch, steps),
            in_specs=[per_b((1, SB_W, LANES)), const((1, LANES)),
                      per_b((1, n_new, SB_W)), per_b((1, n_new, SB_W)),
                      const((n_new + SUBLANES, n_new)), const((page + SUBLANES, page)),
                      const((SB_W, SB_W)), const((1, SB_W))]
                     + [page_spec(u) for u in range(pps)] * 2,
            out_specs=per_b((1, n_new, SB_W)),
            scratch_shapes=[pltpu.VMEM((nq, SB_W), F32), pltpu.VMEM((1, LANES), F32)]),
        out_shape=jax.ShapeDtypeStruct((batch, n_new, SB_W), BF16),
        compiler_params=_params("parallel", "arbitrary"),
        name="sb_paged",
    )(page_table, qbd, bias_row, k_new, v_new, _suffix_matrix(n_new), _suffix_matrix(page), gmat,
      sb_gn.reshape(1, -1), *([k_pages] * pps), *([v_pages] * pps))


def _mix_up_kernel(x_ref, ro_ref, so_ref, gm_ref, sc_ref, sh_ref, nw_ref, wo_ref, wg_ref, wv_ref,
                   cw_ref, cb_ref, h1_ref, h2_ref, x1_ref, g_ref, tail_ref, ext_ref, *, tile, seg):
    i = pl.program_id(0)
    mix = jnp.concatenate([ro_ref[...].astype(BF16), so_ref[...].astype(BF16)], axis=1)
    x1 = x_ref[...] + gm_ref[...] * jnp.dot(mix, wo_ref[...], preferred_element_type=F32)
    x1_ref[...] = x1
    h = (_rms(x1, nw_ref[...]) * (1.0 + sc_ref[...]) + sh_ref[...]).astype(BF16)
    a = jnp.dot(h, wg_ref[...], preferred_element_type=F32)
    bval = jnp.dot(h, wv_ref[...], preferred_element_type=F32)
    tail_ref[...] = a[tile - tail_ref.shape[0]:, :]

    @pl.when(i == 0)
    def _():
        ext_ref[0:SUBLANES, :] = jnp.zeros((SUBLANES, D_FF), F32)

    ext_ref[SUBLANES:, :] = a
    a1 = ext_ref[SUBLANES - 1:SUBLANES - 1 + tile, :]
    a2 = ext_ref[SUBLANES - 2:SUBLANES - 2 + tile, :]
    ext_ref[0:SUBLANES, :] = a[tile - SUBLANES:, :]
    t = (lax.broadcasted_iota(jnp.int32, (tile, 1), 0) + i * tile) % seg
    a1 = jnp.where(t < 1, h1_ref[...], a1)
    a2 = jnp.where(t < 2, h2_ref[...], a2)
    conv = cb_ref[...] + a2 * cw_ref[0:1, :] + a1 * cw_ref[1:2, :] + a * cw_ref[2:3, :]
    g_ref[...] = (_silu(conv) * bval).astype(BF16)


def _mix_up_call(x, ro, so, gm, sc, sh, norm_w, w_out, w_gate, w_val, conv_w, conv_b, hist1, hist2,
                 *, tile, seg, full_tail):
    rows = x.shape[0]
    n = rows // tile
    row = lambda w: pl.BlockSpec((tile, w), lambda i: (i, 0))
    mod = lambda m: _row_spec(m.shape[0], tile, D_MODEL)
    hist = lambda m: _row_spec(m.shape[0], tile, D_FF)
    if full_tail:
        tail_spec, tail_rows = row(D_FF), rows
    else:
        tail_spec, tail_rows = pl.BlockSpec((SUBLANES, D_FF), lambda i: (0, 0)), SUBLANES
    return pl.pallas_call(
        partial(_mix_up_kernel, tile=tile, seg=seg),
        grid=(n,),
        in_specs=[row(D_MODEL), row(RET_W), row(SB_W), mod(gm), mod(sc), mod(sh),
                  _const_spec((1, D_MODEL)), _const_spec((RET_W + SB_W, D_MODEL)),
                  _const_spec((D_MODEL, D_FF)), _const_spec((D_MODEL, D_FF)),
                  _const_spec((CONV_W, D_FF)), _const_spec((1, D_FF)), hist(hist1), hist(hist2)],
        out_specs=[row(D_MODEL), row(D_FF), tail_spec],
        out_shape=[jax.ShapeDtypeStruct((rows, D_MODEL), F32),
                   jax.ShapeDtypeStruct((rows, D_FF), BF16),
                   jax.ShapeDtypeStruct((tail_rows, D_FF), F32)],
        scratch_shapes=[pltpu.VMEM((tile + SUBLANES, D_FF), F32)],
        compiler_params=_params("arbitrary"),
        name="mix_up",
    )(x, ro, so, gm, sc, sh, norm_w.reshape(1, -1), w_out, w_gate, w_val, conv_w,
      conv_b.reshape(1, -1), hist1, hist2)


def _down_kernel(x1_ref, g_ref, gf_ref, wd_ref, nf_ref, y_ref, *, final):
    x2 = x1_ref[...] + gf_ref[...] * jnp.dot(g_ref[...], wd_ref[...], preferred_element_type=F32)
    y_ref[...] = _rms(x2, nf_ref[...]) if final else x2


def _down_call(x1, g, gf, w_down, norm_final, *, tile, final):
    rows = x1.shape[0]
    row = lambda w: pl.BlockSpec((tile, w), lambda i: (i, 0))
    return pl.pallas_call(
        partial(_down_kernel, final=final),
        grid=(rows // tile,),
        in_specs=[row(D_MODEL), row(D_FF), _row_spec(gf.shape[0], tile, D_MODEL),
                  _const_spec((D_FF, D_MODEL)), _const_spec((1, D_MODEL))],
        out_specs=row(D_MODEL),
        out_shape=jax.ShapeDtypeStruct((rows, D_MODEL), F32),
        compiler_params=_params("parallel"),
        name="down",
    )(x1, g, gf, w_down, norm_final.reshape(1, -1))


def _rope_tables(pos):
    half = DK_RET // 2
    inv = ROPE_BASE ** (-jnp.arange(half, dtype=F32) / half)
    ang = pos.astype(F32)[:, None] * inv[None, :]
    cos, sin = jnp.cos(ang), jnp.sin(ang)
    return jnp.concatenate([cos, cos], axis=1), jnp.concatenate([-sin, sin], axis=1)


def _split_ada(ada):
    return [ada[:, i * D_MODEL:(i + 1) * D_MODEL] for i in range(N_ADA)]


def kernel(x_prompt, x_sample, cache_k_pages, cache_v_pages, page_table, state_ret, state_conv,
           c_prompt, c_sample, w_ada, b_ada, norm_mix, w_in, ret_gn, sb_gn, sb_bias, w_out,
           norm_ffn, w_up_gate, w_up_val, conv_w, conv_b, w_down, norm_final):
    b, t, _ = x_prompt.shape
    db, ts, _ = x_sample.shape
    depth = w_in.shape[0]
    n_pages, page = page_table.shape[1], cache_k_pages.shape[2]
    past = n_pages * page
    assert b == 1, "the prompt group is handled as one sequence"

    xp = x_prompt.reshape(b * t, D_MODEL)
    xs = x_sample.reshape(db * ts, D_MODEL)
    cos_p, sin_p = _rope_tables(jnp.arange(t))
    cos_s, sin_s = _rope_tables(jnp.tile(past + jnp.arange(ts), db))
    c_all = jnp.concatenate([c_prompt, c_sample], axis=0)
    pad = (-c_all.shape[0]) % SUBLANES
    c_all = jnp.pad(c_all, ((0, pad), (0, 0)))

    tile_p = min(t, 512)
    bq = bk = min(t, 256)
    outs = {k: [] for k in ("kp", "vp", "ks", "vs", "rp", "rs", "cp", "cs")}
    for l in range(depth):
        ada = _ada_call(c_all, w_ada[l], b_ada[l])
        ada_p = _split_ada(ada[0:b])
        ada_s = _split_ada(jnp.repeat(ada[b:b + db], ts, axis=0))
        w_in_b, w_out_b = w_in[l].astype(BF16), w_out[l].astype(BF16)
        w_gate_b, w_val_b, w_down_b = (w_up_gate[l].astype(BF16), w_up_val[l].astype(BF16),
                                       w_down[l].astype(BF16))

        rq, rk, rv, rg, _, sk, sv, sqt, skb, svt = _inproj_call(
            xp, ada_p[1], ada_p[0], norm_mix[l], w_in_b, cos_p, sin_p, tile=tile_p, transposed=True)
        ro, ret_p = _retention_call(rq, rk, rv, rg, jnp.zeros((b, H_RET, DK_RET, DV_RET), F32),
                                    ret_gn[l], batch=b, seq=t)
        so = _sb_prompt_call(sb_bias[l], sqt, skb, svt, sb_gn[l], bq=bq, bk=bk)
        zeros_hist = jnp.zeros((1, D_FF), F32)
        x1, g, tail = _mix_up_call(xp, ro, so, ada_p[2], ada_p[4], ada_p[3], norm_ffn[l], w_out_b,
                                   w_gate_b, w_val_b, conv_w[l], conv_b[l], zeros_hist, zeros_hist,
                                   tile=min(t, 256), seg=t, full_tail=False)
        final = l == depth - 1
        xp = _down_call(x1, g, ada_p[5], w_down_b, norm_final, tile=min(t, 512), final=final)
        outs["kp"].append(sk.reshape(b, t, H_SB, D_SB))
        outs["vp"].append(sv.reshape(b, t, H_SB, D_SB))
        outs["rp"].append(ret_p)
        outs["cp"].append(tail[SUBLANES - (CONV_W - 1):].reshape(b, CONV_W - 1, D_FF))

        rows_s = db * ts
        rq, rk, rv, rg, sq, sk, sv = _inproj_call(
            xs, ada_s[1], ada_s[0], norm_mix[l], w_in_b, cos_s, sin_s, tile=rows_s, transposed=False)
        ro, ret_s = _retention_call(rq, rk, rv, rg, state_ret[l], ret_gn[l], batch=db, seq=ts)
        q4 = sq.reshape(db, ts, H_SB, D_SB)
        qbd = jnp.einsum("bihd,hg->bhdgi", q4, jnp.eye(H_SB, dtype=F32)).reshape(db, SB_W, H_SB * ts)
        qbd = jnp.pad(qbd, ((0, 0), (0, 0), (0, LANES - H_SB * ts))).astype(BF16)
        bias_row = jnp.pad(jnp.repeat(sb_bias[l], ts), (0, LANES - H_SB * ts)).reshape(1, LANES)
        so = _sb_paged_call(page_table, qbd, bias_row, sk.reshape(db, ts, SB_W), sv.reshape(db, ts, SB_W),
                            cache_k_pages[l].reshape(-1, page, SB_W), cache_v_pages[l].reshape(-1, page, SB_W),
                            sb_gn[l], pps=min(PAGES_PER_STEP, n_pages)).reshape(rows_s, SB_W)
        sc_hist = state_conv[l]
        hist1 = jnp.pad(sc_hist[:, 1:2], ((0, 0), (0, ts - 1), (0, 0))).reshape(rows_s, D_FF)
        hist2 = jnp.pad(sc_hist, ((0, 0), (0, ts - 2), (0, 0))).reshape(rows_s, D_FF)
        x1, g, tail = _mix_up_call(xs, ro, so, ada_s[2], ada_s[4], ada_s[3], norm_ffn[l], w_out_b,
                                   w_gate_b, w_val_b, conv_w[l], conv_b[l], hist1, hist2,
                                   tile=rows_s, seg=ts, full_tail=True)
        xs = _down_call(x1, g, ada_s[5], w_down_b, norm_final, tile=rows_s, final=final)
        outs["ks"].append(sk.reshape(db, ts, H_SB, D_SB))
        outs["vs"].append(sv.reshape(db, ts, H_SB, D_SB))
        outs["rs"].append(ret_s)
        outs["cs"].append(tail.reshape(db, ts, D_FF)[:, ts - (CONV_W - 1):])
    return (xp.reshape(b, t, D_MODEL), xs.reshape(db, ts, D_MODEL),
            jnp.stack(outs["kp"]), jnp.stack(outs["vp"]), jnp.stack(outs["ks"]), jnp.stack(outs["vs"]),
            jnp.stack(outs["rp"]), jnp.stack(outs["rs"]), jnp.stack(outs["cp"]), jnp.stack(outs["cs"]))
```

```python
from functools import partial

import jax
import jax.numpy as jnp
from jax import lax
from jax.experimental import pallas as pl
from jax.experimental.pallas import tpu as pltpu

F32 = jnp.float32
BF16 = jnp.bfloat16

D_MODEL = 1024
H_RET, DK_RET, DV_RET = 4, 128, 128
H_SB, D_SB = 8, 64
RET_W = H_RET * DV_RET
SB_W = H_SB * D_SB
IN_W = 4 * RET_W + 3 * SB_W
D_FF = 2816
CONV_W = 3
N_ADA = 6
ROPE_BASE = 10000.0
EPS = 1e-6
RET_CHUNK = 128
LOG2E = 1.4426950408889634

SUBLANES = 8
LANES = 128
VMEM_LIMIT_BYTES = 56 * 1024 * 1024


def _params(*semantics):
    return pltpu.CompilerParams(dimension_semantics=semantics, vmem_limit_bytes=VMEM_LIMIT_BYTES)


def _silu(x):
    return x / (1.0 + jnp.exp(-x))


def _rms(x, g):
    return x * lax.rsqrt(jnp.mean(x * x, axis=-1, keepdims=True) + EPS) * g


def _row_spec(rows_total, tile, width):
    if rows_total == 1:
        return pl.BlockSpec((1, width), lambda i: (0, 0))
    return pl.BlockSpec((tile, width), lambda i: (i, 0))


def _const_spec(shape):
    return pl.BlockSpec(shape, lambda *_: (0,) * len(shape))


def _ada_kernel(c_ref, w_ref, b_ref, o_ref):
    s = _silu(c_ref[...])
    o_ref[...] = jnp.dot(s, w_ref[...], preferred_element_type=F32,
                         precision=lax.Precision.HIGHEST) + b_ref[...]


def _ada_call(c, w_ada, b_ada):
    rows = c.shape[0]
    tn = 1536
    return pl.pallas_call(
        _ada_kernel,
        grid=(N_ADA * D_MODEL // tn,),
        in_specs=[_const_spec((rows, D_MODEL)),
                  pl.BlockSpec((D_MODEL, tn), lambda j: (0, j)),
                  pl.BlockSpec((1, tn), lambda j: (0, j))],
        out_specs=pl.BlockSpec((rows, tn), lambda j: (0, j)),
        out_shape=jax.ShapeDtypeStruct((rows, N_ADA * D_MODEL), F32),
        compiler_params=_params("parallel"),
        name="ada",
    )(c, w_ada, b_ada.reshape(1, -1))


def _inproj_kernel(x_ref, sc_ref, sh_ref, nw_ref, w_ref, cos_ref, sin_ref,
                   rq_ref, rk_ref, rv_ref, rg_ref, sq_ref, sk_ref, sv_ref, *extra, transposed):
    x = x_ref[...]
    h = _rms(x, nw_ref[...]) * (1.0 + sc_ref[...]) + sh_ref[...]
    proj = jnp.dot(h.astype(BF16), w_ref[...], preferred_element_type=F32)
    cos = cos_ref[...]
    sin = sin_ref[...]

    def rope(t):
        parts = []
        for hh in range(H_RET):
            th = t[:, hh * DK_RET:(hh + 1) * DK_RET]
            parts.append(th * cos + pltpu.roll(th, DK_RET // 2, 1) * sin)
        return jnp.concatenate(parts, axis=1)

    rq_ref[...] = rope(proj[:, 0:RET_W]).astype(rq_ref.dtype)
    rk_ref[...] = (rope(proj[:, RET_W:2 * RET_W]) * (DK_RET ** -0.5)).astype(rk_ref.dtype)
    rv_ref[...] = proj[:, 2 * RET_W:3 * RET_W].astype(rv_ref.dtype)
    rg_ref[...] = proj[:, 3 * RET_W:4 * RET_W]
    o = 4 * RET_W
    sq = proj[:, o:o + SB_W] * (-(D_SB ** -0.5) * LOG2E)
    sk = proj[:, o + SB_W:o + 2 * SB_W]
    sv = proj[:, o + 2 * SB_W:o + 3 * SB_W]
    sk_ref[...] = sk
    sv_ref[...] = sv
    if transposed:
        sqt_ref, skb_ref, svt_ref = extra
        sq_ref[...] = sq
        sqt_ref[...] = sq.T.astype(BF16)
        skb_ref[...] = sk.astype(BF16)
        svt_ref[...] = sv.T.astype(BF16)
    else:
        sq_ref[...] = sq


def _inproj_call(x, sc, sh, norm_w, w_in, cos, sin, *, tile, transposed):
    rows = x.shape[0]
    n = rows // tile
    row = lambda w: pl.BlockSpec((tile, w), lambda i: (i, 0))
    col = lambda w: pl.BlockSpec((w, tile), lambda i: (0, i))
    out_specs = [row(RET_W)] * 4 + [row(SB_W)] * 3
    ret_dtype = BF16 if transposed else F32
    out_shape = ([jax.ShapeDtypeStruct((rows, RET_W), ret_dtype)] * 3
                 + [jax.ShapeDtypeStruct((rows, RET_W), F32)]
                 + [jax.ShapeDtypeStruct((rows, SB_W), F32)] * 3)
    if transposed:
        out_specs += [col(SB_W), row(SB_W), col(SB_W)]
        out_shape += [jax.ShapeDtypeStruct((SB_W, rows), BF16),
                      jax.ShapeDtypeStruct((rows, SB_W), BF16),
                      jax.ShapeDtypeStruct((SB_W, rows), BF16)]
    return pl.pallas_call(
        partial(_inproj_kernel, transposed=transposed),
        grid=(n,),
        in_specs=[row(D_MODEL), _row_spec(sc.shape[0], tile, D_MODEL),
                  _row_spec(sh.shape[0], tile, D_MODEL), _const_spec((1, D_MODEL)),
                  _const_spec((D_MODEL, IN_W)), row(DK_RET), row(DK_RET)],
        out_specs=out_specs,
        out_shape=out_shape,
        compiler_params=_params("parallel"),
        name="inproj",
    )(x, sc, sh, norm_w.reshape(1, -1), w_in, cos, sin)


def _retention_kernel(q_ref, k_ref, v_ref, g_ref, s0_ref, gn_ref, o_ref, s_out_ref, s_scr, *, chunk):
    c = chunk
    ci = pl.program_id(1)

    @pl.when(ci == 0)
    def _():
        s_scr[...] = s0_ref[0]

    row = lax.broadcasted_iota(jnp.int32, (c, c), 0).astype(F32)
    colv = lax.broadcasted_iota(jnp.int32, (c, c), 1).astype(F32)
    diff = row - colv
    ridx = lax.broadcasted_iota(jnp.int32, (c, 1), 0).astype(F32)
    outs = []
    for hh in range(H_RET):
        lg = jnp.log(1.0 - 2.0 ** (-5.0 - hh))
        decay = jnp.where(diff >= 0, jnp.exp(jnp.maximum(diff, 0.0) * lg), 0.0)
        q_decay = jnp.exp((ridx + 1.0) * lg)
        k_decay = jnp.exp((c - 1.0 - ridx) * lg)
        chunk_decay = jnp.exp(c * lg)
        sl = slice(hh * DK_RET, (hh + 1) * DK_RET)
        qh = q_ref[:, sl].astype(BF16)
        kh = k_ref[:, sl].astype(BF16)
        vh = v_ref[:, sl].astype(BF16)
        s = s_scr[hh]
        scores = lax.dot_general(qh, kh, (((1,), (1,)), ((), ())), preferred_element_type=F32) * decay
        o = jnp.dot(scores.astype(BF16), vh, preferred_element_type=F32)
        o = o + jnp.dot((qh.astype(F32) * q_decay).astype(BF16), s.astype(BF16), preferred_element_type=F32)
        kd = (kh.astype(F32) * k_decay).astype(BF16)
        s_new = s * chunk_decay + lax.dot_general(kd, vh, (((0,), (0,)), ((), ())), preferred_element_type=F32)
        s_scr[hh] = s_new
        s_out_ref[0, hh] = s_new
        mu = jnp.mean(o, axis=-1, keepdims=True)
        d = o - mu
        var = jnp.mean(d * d, axis=-1, keepdims=True)
        y = d * lax.rsqrt(var + EPS) * gn_ref[:, sl]
        outs.append(y * _silu(g_ref[:, sl]))
    o_ref[...] = jnp.concatenate(outs, axis=1).astype(o_ref.dtype)


def _retention_call(q, k, v, g, s0, ret_gn, *, batch, seq):
    chunk = RET_CHUNK if seq % RET_CHUNK == 0 else seq
    n = seq // chunk
    row = pl.BlockSpec((chunk, RET_W), lambda b, i: (b * n + i, 0))
    st = pl.BlockSpec((1, H_RET, DK_RET, DV_RET), lambda b, i: (b, 0, 0, 0))
    return pl.pallas_call(
        partial(_retention_kernel, chunk=chunk),
        grid=(batch, n),
        in_specs=[row, row, row, row, st, pl.BlockSpec((1, RET_W), lambda b, i: (0, 0))],
        out_specs=[row, st],
        out_shape=[jax.ShapeDtypeStruct((batch * seq, RET_W), q.dtype),
                   jax.ShapeDtypeStruct((batch, H_RET, DK_RET, DV_RET), F32)],
        scratch_shapes=[pltpu.VMEM((H_RET, DK_RET, DV_RET), F32)],
        compiler_params=_params("parallel", "arbitrary"),
        name="retention",
    )(q, k, v, g, s0, ret_gn.reshape(1, -1))


def _sb_logs(nz, valid):
    sign = jnp.uint32(0x80000000)
    neg_abs = lax.bitcast_convert_type(lax.bitcast_convert_type(nz, jnp.uint32) | sign, F32)
    lg = jnp.log(1.0 + jnp.exp2(neg_abs)) * LOG2E
    log_om = jnp.minimum(nz, 0.0) - lg
    log_b = log_om - nz
    if valid is not None:
        log_om = jnp.where(valid, log_om, 0.0)
    return log_b, log_om


def _suffix_rows(n):
    r = lax.broadcasted_iota(jnp.int32, (n + SUBLANES, n), 0)
    c = lax.broadcasted_iota(jnp.int32, (n + SUBLANES, n), 1)
    return jnp.where((c > r) | (r >= n), 1.0, 0.0).astype(BF16)


def _suffix_lanes(n):
    r = lax.broadcasted_iota(jnp.int32, (n, 2 * n), 0)
    c = lax.broadcasted_iota(jnp.int32, (n, 2 * n), 1)
    return jnp.where((r > c) | (c >= n), 1.0, 0.0).astype(BF16)


SB_UNROLL = 2


def _sb_prompt_kernel(bias_ref, qt_ref, k_ref, vt_ref, suf_ref, gn_ref, o_ref, acc_ref, *, bq, bk):
    p = pl.program_id(0)
    qi = pl.program_id(1)
    r = bq // bk
    suffix = suf_ref[...]
    qt = qt_ref[...]
    head_of_row = lax.broadcasted_iota(jnp.int32, (2 * D_SB, bq), 0) // D_SB
    krow = lax.broadcasted_iota(jnp.int32, (bk, bq), 0)
    qcol = lax.broadcasted_iota(jnp.int32, (bk, bq), 1)
    qms = [jnp.where(head_of_row == s, qt, jnp.zeros((), BF16)) for s in range(2)]
    nbias = [bias_ref[2 * p + s] * -LOG2E for s in range(2)]
    acc_ref[...] = jnp.zeros_like(acc_ref)

    def group(kb_lo, carries, valids):
        n = len(valids)
        k0 = pl.multiple_of(kb_lo * bk, bk)
        keys = k_ref[pl.ds(k0, n * bk), :]
        nz = [jnp.dot(keys, qms[s], preferred_element_type=F32) + nbias[s] for s in range(2)]
        logs = [[_sb_logs(nz[s][blk * bk:(blk + 1) * bk], valids[blk]) for blk in range(n)]
                for s in range(2)]
        sums = [[jnp.dot(suffix, logs[s][blk][1].astype(BF16), preferred_element_type=F32)
                 for blk in range(n)] for s in range(2)]
        out, weights = [], []
        for s in range(2):
            carry = carries[s]
            a_parts = [None] * n
            for blk in reversed(range(n)):
                a = jnp.exp2(logs[s][blk][0] + sums[s][blk][:bk] + carry)
                if valids[blk] is not None:
                    a = jnp.where(valids[blk], a, 0.0)
                a_parts[blk] = a.astype(BF16)
                carry = carry + sums[s][blk][bk:bk + 1]
            out.append(carry)
            weights.append(jnp.concatenate(a_parts, axis=0))
        for s in range(2):
            rows = slice(s * D_SB, (s + 1) * D_SB)
            acc_ref[rows, :] += jnp.dot(vt_ref[rows, pl.ds(k0, n * bk)], weights[s],
                                        preferred_element_type=F32)
        return tuple(out)

    zero = jnp.zeros((1, bq), F32)
    carries = group(qi * r, (zero, zero), [(krow + d * bk) < qcol for d in range(r)])
    u = SB_UNROLL if r % SB_UNROLL == 0 else 1
    lax.fori_loop(0, qi * r // u, lambda j, c: group(qi * r - u * (j + 1), c, [None] * u), carries)

    parts = []
    for s in range(2):
        blk = acc_ref[s * D_SB:(s + 1) * D_SB, :]
        ms = jnp.mean(blk * blk, axis=0, keepdims=True)
        parts.append(blk * lax.rsqrt(ms + EPS))
    o_ref[...] = (jnp.concatenate(parts, axis=0).T * gn_ref[...]).astype(BF16)


def _sb_prompt_call(bias, qt, kb, vt, sb_gn, *, bq, bk):
    seq = kb.shape[0]
    pairs = H_SB // 2
    w = 2 * D_SB
    return pl.pallas_call(
        partial(_sb_prompt_kernel, bq=bq, bk=bk),
        grid_spec=pltpu.PrefetchScalarGridSpec(
            num_scalar_prefetch=0,
            grid=(pairs, seq // bq),
            in_specs=[pl.BlockSpec(memory_space=pltpu.SMEM),
                      pl.BlockSpec((w, bq), lambda p, i: (p, i)),
                      pl.BlockSpec((seq, w), lambda p, i: (0, p)),
                      pl.BlockSpec((w, seq), lambda p, i: (p, 0)),
                      pl.BlockSpec((bk + SUBLANES, bk), lambda p, i: (0, 0)),
                      pl.BlockSpec((1, w), lambda p, i: (0, p))],
            out_specs=pl.BlockSpec((bq, w), lambda p, i: (i, p)),
            scratch_shapes=[pltpu.VMEM((w, bq), F32)]),
        out_shape=jax.ShapeDtypeStruct((seq, SB_W), BF16),
        compiler_params=_params("parallel", "arbitrary"),
        name="sb_prompt",
    )(bias, qt, kb, vt, _suffix_rows(bk), sb_gn.reshape(1, -1))


PAGES_PER_STEP = 8


def _sb_paged_kernel(pt_ref, q_ref, bias_ref, kn_ref, vn_ref, suf_ref, gmat_ref, gn_ref, *refs,
                     n_new, pps):
    k_refs = refs[:pps]
    v_refs = refs[pps:2 * pps]
    o_ref = refs[2 * pps]
    acc_ref, carry_ref = refs[2 * pps + 1:]
    j = pl.program_id(1)
    q = q_ref[0]
    bias = bias_ref[...]
    suffix = suf_ref[...]
    page = suffix.shape[0]

    nq = carry_ref.shape[0]

    def attend(kts, vts, valid, acc_t, carry):
        n = len(kts)
        kt = jnp.concatenate([k.astype(BF16) for k in kts], axis=1)
        nz = jnp.dot(q, kt, preferred_element_type=F32)
        logs = [_sb_logs(nz[:, u * page:(u + 1) * page] + bias, valid) for u in range(n)]
        log_om = jnp.concatenate([lo for _, lo in logs], axis=0).astype(BF16)
        sums = jnp.dot(log_om, suffix, preferred_element_type=F32)
        weights = []
        for u in range(n):
            su = sums[u * nq:(u + 1) * nq]
            a = jnp.exp2(logs[u][0] + su[:, :page] + carry)
            if valid is not None:
                a = jnp.where(valid, a, 0.0)
            weights.append(a.astype(BF16))
            carry = carry + su[:, page:]
        vt = jnp.concatenate([v.astype(BF16) for v in vts], axis=1)
        acc_t = acc_t + lax.dot_general(vt, jnp.concatenate(weights, axis=1), (((1,), (1,)), ((), ())),
                                        preferred_element_type=F32)
        return acc_t, carry

    @pl.when(j == 0)
    def _():
        key = lax.broadcasted_iota(jnp.int32, carry_ref.shape, 1)
        step = lax.broadcasted_iota(jnp.int32, carry_ref.shape, 0) % n_new
        acc0, carry0 = attend([kn_ref[0]], [vn_ref[0]], key < step,
                              jnp.zeros(acc_ref.shape, F32), jnp.zeros(carry_ref.shape, F32))
        acc_ref[...] = acc0
        carry_ref[...] = carry0

    acc_t, carry = attend([r[0] for r in k_refs], [r[0] for r in v_refs], None,
                          acc_ref[...], carry_ref[...])
    acc_ref[...] = acc_t
    carry_ref[...] = carry

    @pl.when(j == pl.num_programs(1) - 1)
    def _():
        acc = jnp.concatenate([acc_t, jnp.zeros_like(acc_t)], axis=1).T[:nq]
        lane_head = lax.broadcasted_iota(jnp.int32, (n_new, SB_W), 1) // D_SB
        so = jnp.zeros((n_new, SB_W), F32)
        for hh in range(H_SB):
            so = so + jnp.where(lane_head == hh, acc[hh * n_new:(hh + 1) * n_new, :], 0.0)
        sq = so * so
        hi = sq.astype(BF16)
        lo = (sq - hi.astype(F32)).astype(BF16)
        ms = (jnp.dot(hi, gmat_ref[...], preferred_element_type=F32)
              + jnp.dot(lo, gmat_ref[...], preferred_element_type=F32)) * (1.0 / D_SB)
        o_ref[0] = (so * lax.rsqrt(ms + EPS) * gn_ref[...]).astype(BF16)


def _sb_paged_call(page_table, q_rows, bias_rows, kt_new, vt_new, kt_pages, vt_pages, sb_gn, *, n_new, pps):
    batch, n_pages = page_table.shape
    page = kt_pages.shape[2]
    nq = H_SB * n_new
    lane_head = jnp.arange(SB_W) // D_SB
    gmat = (lane_head[:, None] == lane_head[None, :]).astype(BF16)

    def page_spec(u):
        return pl.BlockSpec((1, SB_W, page),
                            lambda b, j, pt: (pt[b, n_pages - 1 - (j * pps + u)], 0, 0))

    const = lambda shape: pl.BlockSpec(shape, lambda b, j, pt: (0,) * len(shape))
    per_b = lambda shape: pl.BlockSpec(shape, lambda b, j, pt: (b,) + (0,) * (len(shape) - 1))
    return pl.pallas_call(
        partial(_sb_paged_kernel, n_new=n_new, pps=pps),
        grid_spec=pltpu.PrefetchScalarGridSpec(
            num_scalar_prefetch=1,
            grid=(batch, n_pages // pps),
            in_specs=[per_b((1, nq, SB_W)), const((nq, page)),
                      per_b((1, SB_W, page)), per_b((1, SB_W, page)),
                      const((page, 2 * page)), const((SB_W, SB_W)), const((1, SB_W))]
                     + [page_spec(u) for u in range(pps)] * 2,
            out_specs=per_b((1, n_new, SB_W)),
            scratch_shapes=[pltpu.VMEM((SB_W, nq), F32), pltpu.VMEM((nq, page), F32)]),
        out_shape=jax.ShapeDtypeStruct((batch, n_new, SB_W), BF16),
        compiler_params=_params("parallel", "arbitrary"),
        name="sb_paged",
    )(page_table, q_rows, bias_rows, kt_new, vt_new, _suffix_lanes(page), gmat,
      sb_gn.reshape(1, -1), *([kt_pages] * pps), *([vt_pages] * pps))


def _mix_up_kernel(x_ref, ro_ref, so_ref, gm_ref, sc_ref, sh_ref, nw_ref, wo_ref, wg_ref, wv_ref,
                   cw_ref, cb_ref, h1_ref, h2_ref, x1_ref, g_ref, tail_ref, ext_ref, *, tile, seg):
    i = pl.program_id(0)
    mix = jnp.concatenate([ro_ref[...].astype(BF16), so_ref[...].astype(BF16)], axis=1)
    x1 = x_ref[...] + gm_ref[...] * jnp.dot(mix, wo_ref[...], preferred_element_type=F32)
    x1_ref[...] = x1
    h = (_rms(x1, nw_ref[...]) * (1.0 + sc_ref[...]) + sh_ref[...]).astype(BF16)
    a = jnp.dot(h, wg_ref[...], preferred_element_type=F32)
    bval = jnp.dot(h, wv_ref[...], preferred_element_type=F32)
    tail_ref[...] = a[tile - tail_ref.shape[0]:, :]

    @pl.when(i == 0)
    def _():
        ext_ref[0:SUBLANES, :] = jnp.zeros((SUBLANES, D_FF), F32)

    ext_ref[SUBLANES:, :] = a
    a1 = ext_ref[SUBLANES - 1:SUBLANES - 1 + tile, :]
    a2 = ext_ref[SUBLANES - 2:SUBLANES - 2 + tile, :]
    ext_ref[0:SUBLANES, :] = a[tile - SUBLANES:, :]
    t = (lax.broadcasted_iota(jnp.int32, (tile, 1), 0) + i * tile) % seg
    a1 = jnp.where(t < 1, h1_ref[...], a1)
    a2 = jnp.where(t < 2, h2_ref[...], a2)
    conv = cb_ref[...] + a2 * cw_ref[0:1, :] + a1 * cw_ref[1:2, :] + a * cw_ref[2:3, :]
    g_ref[...] = (_silu(conv) * bval).astype(BF16)


def _mix_up_call(x, ro, so, gm, sc, sh, norm_w, w_out, w_gate, w_val, conv_w, conv_b, hist1, hist2,
                 *, tile, seg, full_tail):
    rows = x.shape[0]
    n = rows // tile
    row = lambda w: pl.BlockSpec((tile, w), lambda i: (i, 0))
    mod = lambda m: _row_spec(m.shape[0], tile, D_MODEL)
    hist = lambda m: _row_spec(m.shape[0], tile, D_FF)
    if full_tail:
        tail_spec, tail_rows = row(D_FF), rows
    else:
        tail_spec, tail_rows = pl.BlockSpec((SUBLANES, D_FF), lambda i: (0, 0)), SUBLANES
    return pl.pallas_call(
        partial(_mix_up_kernel, tile=tile, seg=seg),
        grid=(n,),
        in_specs=[row(D_MODEL), row(RET_W), row(SB_W), mod(gm), mod(sc), mod(sh),
                  _const_spec((1, D_MODEL)), _const_spec((RET_W + SB_W, D_MODEL)),
                  _const_spec((D_MODEL, D_FF)), _const_spec((D_MODEL, D_FF)),
                  _const_spec((CONV_W, D_FF)), _const_spec((1, D_FF)), hist(hist1), hist(hist2)],
        out_specs=[row(D_MODEL), row(D_FF), tail_spec],
        out_shape=[jax.ShapeDtypeStruct((rows, D_MODEL), F32),
                   jax.ShapeDtypeStruct((rows, D_FF), BF16),
                   jax.ShapeDtypeStruct((tail_rows, D_FF), F32)],
        scratch_shapes=[pltpu.VMEM((tile + SUBLANES, D_FF), F32)],
        compiler_params=_params("arbitrary"),
        name="mix_up",
    )(x, ro, so, gm, sc, sh, norm_w.reshape(1, -1), w_out, w_gate, w_val, conv_w,
      conv_b.reshape(1, -1), hist1, hist2)


def _down_kernel(x1_ref, g_ref, gf_ref, wd_ref, nf_ref, y_ref, *, final):
    x2 = x1_ref[...] + gf_ref[...] * jnp.dot(g_ref[...], wd_ref[...], preferred_element_type=F32)
    y_ref[...] = _rms(x2, nf_ref[...]) if final else x2


def _down_call(x1, g, gf, w_down, norm_final, *, tile, final):
    rows = x1.shape[0]
    row = lambda w: pl.BlockSpec((tile, w), lambda i: (i, 0))
    return pl.pallas_call(
        partial(_down_kernel, final=final),
        grid=(rows // tile,),
        in_specs=[row(D_MODEL), row(D_FF), _row_spec(gf.shape[0], tile, D_MODEL),
                  _const_spec((D_FF, D_MODEL)), _const_spec((1, D_MODEL))],
        out_specs=row(D_MODEL),
        out_shape=jax.ShapeDtypeStruct((rows, D_MODEL), F32),
        compiler_params=_params("parallel"),
        name="down",
    )(x1, g, gf, w_down, norm_final.reshape(1, -1))


def _rope_tables(pos):
    half = DK_RET // 2
    inv = ROPE_BASE ** (-jnp.arange(half, dtype=F32) / half)
    ang = pos.astype(F32)[:, None] * inv[None, :]
    cos, sin = jnp.cos(ang), jnp.sin(ang)
    return jnp.concatenate([cos, cos], axis=1), jnp.concatenate([-sin, sin], axis=1)


def _split_ada(ada):
    return [ada[:, i * D_MODEL:(i + 1) * D_MODEL] for i in range(N_ADA)]


def kernel(x_prompt, x_sample, cache_k_pages, cache_v_pages, page_table, state_ret, state_conv,
           c_prompt, c_sample, w_ada, b_ada, norm_mix, w_in, ret_gn, sb_gn, sb_bias, w_out,
           norm_ffn, w_up_gate, w_up_val, conv_w, conv_b, w_down, norm_final):
    b, t, _ = x_prompt.shape
    db, ts, _ = x_sample.shape
    depth = w_in.shape[0]
    n_pages, page = page_table.shape[1], cache_k_pages.shape[2]
    past = n_pages * page
    assert b == 1, "the prompt group is handled as one sequence"

    xp = x_prompt.reshape(b * t, D_MODEL)
    xs = x_sample.reshape(db * ts, D_MODEL)
    cos_p, sin_p = _rope_tables(jnp.arange(t))
    cos_s, sin_s = _rope_tables(jnp.tile(past + jnp.arange(ts), db))
    c_all = jnp.concatenate([c_prompt, c_sample], axis=0)
    pad = (-c_all.shape[0]) % SUBLANES
    c_all = jnp.pad(c_all, ((0, pad), (0, 0)))

    tile_p = min(t, 512)
    bq, bk = min(t, 512), min(t, 256)
    outs = {k: [] for k in ("kp", "vp", "ks", "vs", "rp", "rs", "cp", "cs")}
    for l in range(depth):
        ada = _ada_call(c_all, w_ada[l], b_ada[l])
        ada_p = _split_ada(ada[0:b])
        ada_s = _split_ada(jnp.repeat(ada[b:b + db], ts, axis=0))
        w_in_b, w_out_b = w_in[l].astype(BF16), w_out[l].astype(BF16)
        w_gate_b, w_val_b, w_down_b = (w_up_gate[l].astype(BF16), w_up_val[l].astype(BF16),
                                       w_down[l].astype(BF16))

        rq, rk, rv, rg, _, sk, sv, sqt, skb, svt = _inproj_call(
            xp, ada_p[1], ada_p[0], norm_mix[l], w_in_b, cos_p, sin_p, tile=tile_p, transposed=True)
        ro, ret_p = _retention_call(rq, rk, rv, rg, jnp.zeros((b, H_RET, DK_RET, DV_RET), F32),
                                    ret_gn[l], batch=b, seq=t)
        so = _sb_prompt_call(sb_bias[l], sqt, skb, svt, sb_gn[l], bq=bq, bk=bk)
        zeros_hist = jnp.zeros((1, D_FF), F32)
        x1, g, tail = _mix_up_call(xp, ro, so, ada_p[2], ada_p[4], ada_p[3], norm_ffn[l], w_out_b,
                                   w_gate_b, w_val_b, conv_w[l], conv_b[l], zeros_hist, zeros_hist,
                                   tile=min(t, 256), seg=t, full_tail=False)
        final = l == depth - 1
        xp = _down_call(x1, g, ada_p[5], w_down_b, norm_final, tile=min(t, 512), final=final)
        outs["kp"].append(sk.reshape(b, t, H_SB, D_SB))
        outs["vp"].append(sv.reshape(b, t, H_SB, D_SB))
        outs["rp"].append(ret_p)
        outs["cp"].append(tail[SUBLANES - (CONV_W - 1):].reshape(b, CONV_W - 1, D_FF))

        rows_s = db * ts
        rq, rk, rv, rg, sq, sk, sv = _inproj_call(
            xs, ada_s[1], ada_s[0], norm_mix[l], w_in_b, cos_s, sin_s, tile=rows_s, transposed=False)
        ro, ret_s = _retention_call(rq, rk, rv, rg, state_ret[l], ret_gn[l], batch=db, seq=ts)
        q4 = sq.reshape(db, ts, H_SB, D_SB)
        q_rows = jnp.einsum("bihd,hg->bhigd", q4, jnp.eye(H_SB, dtype=F32))
        q_rows = q_rows.reshape(db, H_SB * ts, SB_W).astype(BF16)
        bias_rows = jnp.broadcast_to((jnp.repeat(sb_bias[l], ts) * -LOG2E)[:, None], (H_SB * ts, page))
        as_page = lambda a: jnp.pad(jnp.swapaxes(a.reshape(db, ts, SB_W), 1, 2),
                                    ((0, 0), (0, 0), (0, page - ts)))
        page_view = lambda c: jnp.transpose(c, (0, 2, 3, 1)).reshape(-1, SB_W, page)
        so = _sb_paged_call(page_table, q_rows, bias_rows, as_page(sk), as_page(sv),
                            page_view(cache_k_pages[l]), page_view(cache_v_pages[l]), sb_gn[l],
                            n_new=ts, pps=min(PAGES_PER_STEP, n_pages)).reshape(rows_s, SB_W)
        sc_hist = state_conv[l]
        hist1 = jnp.pad(sc_hist[:, 1:2], ((0, 0), (0, ts - 1), (0, 0))).reshape(rows_s, D_FF)
        hist2 = jnp.pad(sc_hist, ((0, 0), (0, ts - 2), (0, 0))).reshape(rows_s, D_FF)
        x1, g, tail = _mix_up_call(xs, ro, so, ada_s[2], ada_s[4], ada_s[3], norm_ffn[l], w_out_b,
                                   w_gate_b, w_val_b, conv_w[l], conv_b[l], hist1, hist2,
                                   tile=rows_s, seg=ts, full_tail=True)
        xs = _down_call(x1, g, ada_s[5], w_down_b, norm_final, tile=rows_s, final=final)
        outs["ks"].append(sk.reshape(db, ts, H_SB, D_SB))
        outs["vs"].append(sv.reshape(db, ts, H_SB, D_SB))
        outs["rs"].append(ret_s)
        outs["cs"].append(tail.reshape(db, ts, D_FF)[:, ts - (CONV_W - 1):])
    return (xp.reshape(b, t, D_MODEL), xs.reshape(db, ts, D_MODEL),
            jnp.stack(outs["kp"]), jnp.stack(outs["vp"]), jnp.stack(outs["ks"]), jnp.stack(outs["vs"]),
            jnp.stack(outs["rp"]), jnp.stack(outs["rs"]), jnp.stack(outs["cp"]), jnp.stack(outs["cs"]))
```
